```python
import jax, jax.numpy as jnp
from jax import lax
import numpy as np

D_MODEL = 1024
BATCH = 1
SEQ = 16384
DEPTH = 1

RET_HEADS = 4
RET_DK = 128
RET_DV = 256
RET_CHUNK = 128
RET_QK_W = RET_HEADS * RET_DK
RET_V_W = RET_HEADS * RET_DV

ATT_Q_HEADS = 16
ATT_KV_HEADS = 4
ATT_HEAD_DIM = 64
ATT_Q_W = ATT_Q_HEADS * ATT_HEAD_DIM
ATT_KV_W = ATT_KV_HEADS * ATT_HEAD_DIM
WINDOW = 128
ATT_BLOCK = 128

SPLIT_SIZES = (RET_QK_W, RET_QK_W, RET_V_W, RET_V_W,
               ATT_Q_W, ATT_KV_W, ATT_KV_W, ATT_Q_W,
               D_MODEL, D_MODEL)
D_IN = RET_QK_W * 2 + RET_V_W * 2 + ATT_Q_W * 2 + ATT_KV_W * 2 + D_MODEL * 2

RMS_EPS = 1e-6
GN_EPS = 1e-6

kernel_name = "hybrid_retention_window_gqa_gated"


def _rmsnorm(x, g):
    xf = x.astype(jnp.float32)
    y = xf * lax.rsqrt(jnp.mean(xf * xf, axis=-1, keepdims=True) + RMS_EPS)
    return (y * g.astype(jnp.float32)).astype(x.dtype)


def _group_norm(y):
    mu = jnp.mean(y, axis=-1, keepdims=True)
    var = jnp.mean(jnp.square(y - mu), axis=-1, keepdims=True)
    return (y - mu) * lax.rsqrt(var + GN_EPS)


def _split_columns(proj):
    parts = []
    start = 0
    for size in SPLIT_SIZES:
        parts.append(proj[..., start:start + size])
        start += size
    return parts


def _retention_dir(q, k, v, log_gamma, include_diag):
    B, S, H, dk = q.shape
    dv = v.shape[-1]
    C = RET_CHUNK
    n = S // C
    f32 = jnp.float32
    qc = q.astype(f32).reshape(B, n, C, H, dk)
    kc = k.astype(f32).reshape(B, n, C, H, dk)
    vc = v.astype(f32).reshape(B, n, C, H, dv)
    lg = log_gamma.astype(f32)
    pos = jnp.arange(C, dtype=f32)
    diff = pos[:, None] - pos[None, :]
    mask = (diff >= 0) if include_diag else (diff > 0)
    decay_intra = jnp.where(mask[None], jnp.exp(lg[:, None, None] * jnp.maximum(diff, 0.0)[None]), 0.0)
    scores = jnp.einsum('bnihd,bnjhd->bnhij', qc, kc) * decay_intra[None, None]
    intra = jnp.einsum('bnhij,bnjhe->bnihe', scores, vc)
    k_dec = jnp.exp(lg[None, :] * (C - 1 - pos)[:, None])
    kv = jnp.einsum('bnjhd,jh,bnjhe->nbhde', kc, k_dec, vc)
    chunk_decay = jnp.exp(lg * C)[None, :, None, None]

    def step(state, kv_c):
        return state * chunk_decay + kv_c, state

    _, prev = lax.scan(step, jnp.zeros((B, H, dk, dv), f32), kv)
    q_dec = jnp.exp(lg[None, :] * (pos + 1.0)[:, None])
    cross = jnp.einsum('bnihd,ih,nbhde->bnihe', qc, q_dec, prev)
    return (intra + cross).reshape(B, S, H, dv)


def _bidir_retention(q, k, v, log_decay):
    fwd = _retention_dir(q, k, v, log_decay[0], True)
    bwd = _retention_dir(jnp.flip(q, 1), jnp.flip(k, 1), jnp.flip(v, 1), log_decay[1], False)
    return fwd + jnp.flip(bwd, 1)


def _window_attention(q, k, v, sink):
    B, S, Hq, d = q.shape
    Hkv = k.shape[2]
    G = Hq // Hkv
    L = ATT_BLOCK
    n = S // L
    f32 = jnp.float32
    slopes = jnp.exp2(-8.0 * (jnp.arange(Hq, dtype=f32) + 1.0) / Hq).reshape(Hkv, G)
    sink_b = sink.astype(f32).reshape(Hkv, G)[None, :, :, None, None]
    kp = jnp.pad(k, ((0, 0), (L, L), (0, 0), (0, 0)))
    vp = jnp.pad(v, ((0, 0), (L, L), (0, 0), (0, 0)))
    qpos = jnp.arange(L)
    kofs = jnp.arange(3 * L) - L
    rel = jnp.abs(kofs[None, :] - qpos[:, None])
    alibi = -slopes[:, :, None, None] * rel.astype(f32)[None, None]
    scale = d ** -0.5

    def block(i):
        qi = lax.dynamic_slice_in_dim(q, i * L, L, axis=1).reshape(B, L, Hkv, G, d)
        ki = lax.dynamic_slice_in_dim(kp, i * L, 3 * L, axis=1)
        vi = lax.dynamic_slice_in_dim(vp, i * L, 3 * L, axis=1)
        s = jnp.einsum('bqhgd,bkhd->bhgqk', qi, ki).astype(f32) * scale + alibi[None]
        abs_k = i * L + kofs
        valid = (rel <= WINDOW) & ((abs_k >= 0) & (abs_k < S))[None, :]
        s = jnp.where(valid[None, None, None], s, -jnp.inf)
        m = jnp.maximum(jnp.max(s, axis=-1, keepdims=True), sink_b)
        p = jnp.exp(s - m)
        denom = jnp.sum(p, axis=-1, keepdims=True) + jnp.exp(sink_b - m)
        o = jnp.einsum('bhgqk,bkhd->bqhgd', (p / denom).astype(v.dtype), vi)
        return o.reshape(B, L, Hq * d)

    out = lax.map(block, jnp.arange(n))
    return jnp.transpose(out, (1, 0, 2, 3)).reshape(B, S, Hq * d)


def _layer(x, norm_g, w_in, ret_log_decay, q_norm_g, k_norm_g, attn_sink, w_ret_o, w_attn_o, w_out):
    B, S, _ = x.shape
    h = _rmsnorm(x, norm_g)
    proj = jnp.einsum('bsd,de->bse', h, w_in)
    rq, rk, rv, rg, aq, ak, av, ag, mr, ma = _split_columns(proj)
    rq = rq.reshape(B, S, RET_HEADS, RET_DK)
    rk = rk.reshape(B, S, RET_HEADS, RET_DK) * (RET_DK ** -0.5)
    rv = rv.reshape(B, S, RET_HEADS, RET_DV)
    ret = _group_norm(_bidir_retention(rq, rk, rv, ret_log_decay))
    ret = ret.reshape(B, S, RET_V_W).astype(x.dtype)
    y_r = jnp.einsum('bse,ed->bsd', jax.nn.silu(rg) * ret, w_ret_o)
    aq = _rmsnorm(aq.reshape(B, S, ATT_Q_HEADS, ATT_HEAD_DIM), q_norm_g)
    ak = _rmsnorm(ak.reshape(B, S, ATT_KV_HEADS, ATT_HEAD_DIM), k_norm_g)
    av = av.reshape(B, S, ATT_KV_HEADS, ATT_HEAD_DIM)
    att = _window_attention(aq, ak, av, attn_sink)
    y_a = jnp.einsum('bse,ed->bsd', jax.nn.silu(ag) * att, w_attn_o)
    merged = jax.nn.sigmoid(mr) * y_r + jax.nn.sigmoid(ma) * y_a
    return x + jnp.einsum('bsd,de->bse', merged, w_out)


def setup_inputs(seed: int = 0) -> dict:
    key = jax.random.key(seed)
    ks = jax.random.split(key, 11)
    f32 = jnp.float32
    x = jax.random.normal(ks[0], (BATCH, SEQ, D_MODEL), f32)
    norm_g = 1.0 + 0.02 * jax.random.normal(ks[1], (DEPTH, D_MODEL), f32)
    w_in = jax.random.normal(ks[2], (DEPTH, D_MODEL, D_IN), f32) * D_MODEL ** -0.5
    base = jnp.log(1.0 - jnp.exp2(-5.0 - jnp.arange(RET_HEADS, dtype=f32)))
    ret_log_decay = base[None, None, :] * jnp.exp(0.1 * jax.random.normal(ks[3], (DEPTH, 2, RET_HEADS), f32))
    q_norm_g = 1.0 + 0.02 * jax.random.normal(ks[4], (DEPTH, ATT_HEAD_DIM), f32)
    k_norm_g = 1.0 + 0.02 * jax.random.normal(ks[5], (DEPTH, ATT_HEAD_DIM), f32)
    attn_sink = 0.5 * jax.random.normal(ks[6], (DEPTH, ATT_Q_HEADS), f32)
    w_ret_o = jax.random.normal(ks[7], (DEPTH, RET_V_W, D_MODEL), f32) * RET_V_W ** -0.5
    w_attn_o = jax.random.normal(ks[8], (DEPTH, ATT_Q_W, D_MODEL), f32) * ATT_Q_W ** -0.5
    w_out = jax.random.normal(ks[9], (DEPTH, D_MODEL, D_MODEL), f32) * D_MODEL ** -0.5
    return {"x": x, "norm_g": norm_g, "w_in": w_in, "ret_log_decay": ret_log_decay,
            "q_norm_g": q_norm_g, "k_norm_g": k_norm_g, "attn_sink": attn_sink,
            "w_ret_o": w_ret_o, "w_attn_o": w_attn_o, "w_out": w_out}


def reference(x, norm_g, w_in, ret_log_decay, q_norm_g, k_norm_g, attn_sink, w_ret_o, w_attn_o, w_out):
    for l in range(DEPTH):
        x = _layer(x, norm_g[l], w_in[l], ret_log_decay[l], q_norm_g[l], k_norm_g[l],
                   attn_sink[l], w_ret_o[l], w_attn_o[l], w_out[l])
    return x
```

```python
import functools

import jax
import jax.numpy as jnp
from jax.experimental import pallas as pl
from jax.experimental.pallas import tpu as pltpu

D_MODEL = 1024
RET_HEADS = 4
RET_DK = 128
RET_DV = 256
CHUNK = 128
ATT_Q_HEADS = 16
ATT_KV_HEADS = 4
ATT_HEAD_DIM = 64
RET_QK_W = RET_HEADS * RET_DK
RET_V_W = RET_HEADS * RET_DV
ATT_Q_W = ATT_Q_HEADS * ATT_HEAD_DIM
ATT_KV_W = ATT_KV_HEADS * ATT_HEAD_DIM
KV_EXT_W = ATT_KV_HEADS * 2 * 128
RMS_EPS = 1e-6
GN_EPS = 1e-6

_OFF = {}
_o = 0
for _name, _w in (("rq", RET_QK_W), ("rk", RET_QK_W), ("rv", RET_V_W), ("rg", RET_V_W),
                  ("aq", ATT_Q_W), ("ak", ATT_KV_W), ("av", ATT_KV_W), ("ag", ATT_Q_W),
                  ("mr", D_MODEL), ("ma", D_MODEL)):
    _OFF[_name] = (_o, _o + _w)
    _o += _w
D_IN = _o

PROJ_ROWS = 256
BSTATE_ROWS = 512
MIX_ROWS = 256
VMEM_LIMIT_BYTES = 56 * 1024 * 1024

BF16 = jnp.bfloat16
F32 = jnp.float32


def _dot(a, b):
    return jnp.dot(a, b, preferred_element_type=F32)


def _dot_nt(a, b):
    return jax.lax.dot_general(a, b, (((1,), (1,)), ((), ())), preferred_element_type=F32)


def _dot_tn(a, b):
    return jax.lax.dot_general(a, b, (((0,), (0,)), ((), ())), preferred_element_type=F32)


def _split_hi_lo(x):
    hi = x.astype(BF16)
    lo = (x - hi.astype(F32)).astype(BF16)
    return hi, lo


def _head_rmsnorm(x, seg_mean, gain):
    outs = []
    for j in range(x.shape[1] // 256):
        xs = x[:, j * 256:(j + 1) * 256]
        hi, lo = _split_hi_lo(xs * xs)
        ms = _dot(hi, seg_mean) + _dot(lo, seg_mean)
        outs.append(xs * jax.lax.rsqrt(ms + RMS_EPS))
    y = outs[0] if len(outs) == 1 else jnp.concatenate(outs, axis=1)
    return y * gain


def _kv_extend(x):
    lane = jax.lax.broadcasted_iota(jnp.int32, (x.shape[0], 128), 1)
    low = lane < 64
    zero = jnp.zeros((x.shape[0], 128), F32)
    parts = []
    for j in range(2):
        col = x[:, j * 128:(j + 1) * 128]
        rolled = pltpu.roll(col, 64, axis=1)
        parts += [jnp.where(low, col, zero), jnp.where(low, zero, rolled),
                  jnp.where(low, rolled, zero), jnp.where(low, zero, col)]
    return jnp.concatenate(parts, axis=1)


def _proj_kernel(x_ref, g_ref, w_ref, segm_ref, qg_ref, kg_ref,
                 rq_ref, rk_ref, rv_ref, rg_ref, aq_ref, akx_ref, avx_ref, ag_ref, mr_ref, ma_ref):
    x = x_ref[...]
    ms = jnp.mean(x * x, axis=-1, keepdims=True)
    h = (x * jax.lax.rsqrt(ms + RMS_EPS) * g_ref[...]).astype(BF16)

    def proj(name):
        a, b = _OFF[name]
        return _dot(h, w_ref[:, a:b])

    rq_ref[...] = proj("rq").astype(BF16)
    rk_ref[...] = (proj("rk") * (RET_DK ** -0.5)).astype(BF16)
    rv_ref[...] = proj("rv").astype(BF16)
    rg_ref[...] = proj("rg")
    segm = segm_ref[...]
    aq_ref[...] = _head_rmsnorm(proj("aq"), segm, qg_ref[...]).astype(BF16)
    akx_ref[...] = _kv_extend(_head_rmsnorm(proj("ak"), segm, kg_ref[...])).astype(BF16)
    avx_ref[...] = _kv_extend(proj("av")).astype(BF16)
    ag_ref[...] = proj("ag")
    mr_ref[...] = proj("mr")
    ma_ref[...] = proj("ma")


def _bstate_kernel(lg_ref, rk_ref, rv_ref, sb_ref, state_ref):
    @pl.when(pl.program_id(0) == 0)
    def _():
        state_ref[...] = jnp.zeros_like(state_ref)

    n_chunks = BSTATE_ROWS // CHUNK
    pos = jax.lax.broadcasted_iota(jnp.int32, (CHUNK, 1), 0).astype(F32)
    for c in range(n_chunks - 1, -1, -1):
        rows = slice(c * CHUNK, (c + 1) * CHUNK)
        for hd in range(RET_HEADS):
            lgb = lg_ref[1, hd]
            st = state_ref[hd]
            sb_ref[c, hd] = st.astype(BF16)
            k = rk_ref[rows, hd * RET_DK:(hd + 1) * RET_DK].astype(F32)
            kb = (k * jnp.exp(lgb * pos)).astype(BF16)
            v = rv_ref[rows, hd * RET_DV:(hd + 1) * RET_DV]
            kv = _dot_tn(kb, v)
            decay = jnp.exp(jnp.full((1, RET_DV), lgb * float(CHUNK), F32))
            state_ref[hd] = st * decay + kv


def _sigmoid(x):
    return 1.0 / (1.0 + jnp.exp(-x))


def _mix_kernel(lg_ref, sink_ref,
                x_ref, rq_ref, rk_ref, rv_ref, rg_ref, sb_ref,
                aq_ref, akp_ref, akc_ref, akn_ref, avp_ref, avc_ref, avn_ref, ag_ref,
                mr_ref, ma_ref, wro_ref, wao_ref, wout_ref,
                out_ref,
                sf_ref, kbuf_ref, vbuf_ref, ret_ref, att_ref, *, seq_len):
    step = pl.program_id(0)
    n_chunks = MIX_ROWS // CHUNK

    @pl.when(step == 0)
    def _():
        sf_ref[...] = jnp.zeros_like(sf_ref)

    kbuf_ref[0:CHUNK] = akp_ref[...]
    kbuf_ref[CHUNK:CHUNK + MIX_ROWS] = akc_ref[...]
    kbuf_ref[CHUNK + MIX_ROWS:] = akn_ref[...]
    vbuf_ref[0:CHUNK] = avp_ref[...]
    vbuf_ref[CHUNK:CHUNK + MIX_ROWS] = avc_ref[...]
    vbuf_ref[CHUNK + MIX_ROWS:] = avn_ref[...]

    row_i = jax.lax.broadcasted_iota(jnp.int32, (CHUNK, CHUNK), 0)
    col_i = jax.lax.broadcasted_iota(jnp.int32, (CHUNK, CHUNK), 1)
    dist = (row_i - col_i).astype(F32)
    pos = jax.lax.broadcasted_iota(jnp.int32, (CHUNK, 1), 0).astype(F32)

    qrow = jax.lax.broadcasted_iota(jnp.int32, (CHUNK, 3 * CHUNK), 0)
    kcol = jax.lax.broadcasted_iota(jnp.int32, (CHUNK, 3 * CHUNK), 1)
    rel_i = jnp.abs(kcol - CHUNK - qrow)
    rel = rel_i.astype(F32)
    lane128 = jax.lax.broadcasted_iota(jnp.int32, (CHUNK, 128), 1)
    scale = ATT_HEAD_DIM ** -0.5

    for c in range(n_chunks):
        rows = slice(c * CHUNK, (c + 1) * CHUNK)

        for hd in range(RET_HEADS):
            lgf = lg_ref[0, hd]
            lgb = lg_ref[1, hd]
            q = rq_ref[rows, hd * RET_DK:(hd + 1) * RET_DK]
            k = rk_ref[rows, hd * RET_DK:(hd + 1) * RET_DK]
            v = rv_ref[rows, hd * RET_DV:(hd + 1) * RET_DV]
            decay = jnp.where(dist >= 0.0, jnp.exp(lgf * jnp.maximum(dist, 0.0)),
                              jnp.exp(lgb * jnp.maximum(-dist, 0.0)))
            scores = _dot_nt(q, k) * decay
            intra = _dot(scores.astype(BF16), v)
            qf = q.astype(F32)
            q_dec = jnp.concatenate([qf * jnp.exp(lgf * (pos + 1.0)),
                                     qf * jnp.exp(lgb * (float(CHUNK) - pos))], axis=1).astype(BF16)
            s_f = sf_ref[hd]
            states = jnp.concatenate([s_f.astype(BF16), sb_ref[c, hd]], axis=0)
            cross = _dot(q_dec, states)
            k_dec = (k.astype(F32) * jnp.exp(lgf * (float(CHUNK - 1) - pos))).astype(BF16)
            chunk_decay = jnp.exp(jnp.full((1, RET_DV), lgf * float(CHUNK), F32))
            sf_ref[hd] = s_f * chunk_decay + _dot_tn(k_dec, v)
            o = intra + cross
            mu = jnp.mean(o, axis=-1, keepdims=True)
            d = o - mu
            var = jnp.mean(d * d, axis=-1, keepdims=True)
            y = d * jax.lax.rsqrt(var + GN_EPS)
            g = rg_ref[rows, hd * RET_DV:(hd + 1) * RET_DV]
            ret_ref[rows, hd * RET_DV:(hd + 1) * RET_DV] = (g * _sigmoid(g) * y).astype(BF16)

        abs_k = (step * MIX_ROWS + (c - 1) * CHUNK) + kcol
        valid = (rel_i <= CHUNK) & (abs_k >= 0) & (abs_k < seq_len)
        win = slice(c * CHUNK, (c + 3) * CHUNK)
        for grp in range(ATT_KV_HEADS):
            lo = slice(grp * 256, grp * 256 + 128)
            hi = slice(grp * 256 + 128, grp * 256 + 256)
            kk = jnp.concatenate([kbuf_ref[win, lo], kbuf_ref[win, hi]], axis=0)
            vv = jnp.concatenate([vbuf_ref[win, lo], vbuf_ref[win, hi]], axis=0)
            q2 = jnp.concatenate([aq_ref[rows, lo], aq_ref[rows, hi]], axis=0)
            s_all = _dot_nt(q2, kk)
            p_rows = []
            inv_rows = []
            for r in range(2):
                p_cols = []
                invs = []
                for e in range(2):
                    head = grp * 4 + r * 2 + e
                    slope = 2.0 ** (-8.0 * (head + 1) / ATT_Q_HEADS)
                    sink = sink_ref[head]
                    s = s_all[r * CHUNK:(r + 1) * CHUNK, e * 3 * CHUNK:(e + 1) * 3 * CHUNK]
                    s = jnp.where(valid, s * scale - slope * rel, -jnp.inf)
                    m = jnp.maximum(jnp.max(s, axis=-1, keepdims=True), sink)
                    p = jnp.exp(s - m)
                    denom = jnp.sum(p, axis=-1, keepdims=True) + jnp.exp(sink - m)
                    p_cols.append(p.astype(BF16))
                    invs.append(1.0 / denom)
                p_rows.append(jnp.concatenate(p_cols, axis=1))
                inv_rows.append(jnp.where(lane128 < 64, invs[0], invs[1]))
            p_all = jnp.concatenate(p_rows, axis=0)
            o2 = _dot(p_all, vv)
            for r in range(2):
                cols = slice(grp * 256 + r * 128, grp * 256 + (r + 1) * 128)
                ag = ag_ref[rows, cols]
                att = o2[r * CHUNK:(r + 1) * CHUNK] * inv_rows[r]
                att_ref[rows, cols] = (ag * _sigmoid(ag) * att).astype(BF16)

    y_r = _dot(ret_ref[...], wro_ref[...])
    y_a = _dot(att_ref[...], wao_ref[...])
    merged = (_sigmoid(mr_ref[...]) * y_r + _sigmoid(ma_ref[...]) * y_a).astype(BF16)
    out_ref[...] = x_ref[...] + _dot(merged, wout_ref[...])


def _const_spec(shape):
    return pl.BlockSpec(shape, lambda i: (0,) * len(shape), pipeline_mode=pl.Buffered(1))


def _smem_spec():
    return pl.BlockSpec(memory_space=pltpu.SMEM)


def _seg_mean_matrix():
    r = jnp.arange(256)[:, None] // ATT_HEAD_DIM
    c = jnp.arange(256)[None, :] // ATT_HEAD_DIM
    return jnp.where(r == c, 1.0 / ATT_HEAD_DIM, 0.0).astype(BF16)


def _layer(x2, norm_g, w_in, lg, q_g, k_g, sink, w_ret_o, w_attn_o, w_out):
    S = x2.shape[0]
    n_chunks = S // CHUNK
    params = functools.partial(pltpu.CompilerParams, vmem_limit_bytes=VMEM_LIMIT_BYTES)

    tp = PROJ_ROWS
    row = lambda w: pl.BlockSpec((tp, w), lambda i: (i, 0))
    outs = pl.pallas_call(
        _proj_kernel,
        grid=(S // tp,),
        in_specs=[row(D_MODEL), _const_spec((1, D_MODEL)), _const_spec((D_MODEL, D_IN)),
                  _const_spec((256, 256)), _const_spec((1, ATT_Q_W)), _const_spec((1, ATT_KV_W))],
        out_specs=[row(RET_QK_W), row(RET_QK_W), row(RET_V_W), row(RET_V_W), row(ATT_Q_W),
                   row(KV_EXT_W), row(KV_EXT_W), row(ATT_Q_W), row(D_MODEL), row(D_MODEL)],
        out_shape=[jax.ShapeDtypeStruct((S, RET_QK_W), BF16), jax.ShapeDtypeStruct((S, RET_QK_W), BF16),
                   jax.ShapeDtypeStruct((S, RET_V_W), BF16), jax.ShapeDtypeStruct((S, RET_V_W), F32),
                   jax.ShapeDtypeStruct((S, ATT_Q_W), BF16), jax.ShapeDtypeStruct((S, KV_EXT_W), BF16),
                   jax.ShapeDtypeStruct((S, KV_EXT_W), BF16), jax.ShapeDtypeStruct((S, ATT_Q_W), F32),
                   jax.ShapeDtypeStruct((S, D_MODEL), F32), jax.ShapeDtypeStruct((S, D_MODEL), F32)],
        compiler_params=params(dimension_semantics=("arbitrary",)),
        name="proj",
    )(x2, norm_g.reshape(1, D_MODEL), w_in.astype(BF16), _seg_mean_matrix(),
      jnp.tile(q_g.reshape(1, ATT_HEAD_DIM), (1, ATT_Q_HEADS)),
      jnp.tile(k_g.reshape(1, ATT_HEAD_DIM), (1, ATT_KV_HEADS)))
    rq, rk, rv, rg, aq, akx, avx, ag, mr, ma = outs

    tb = BSTATE_ROWS
    nb = S // tb
    cb = tb // CHUNK
    sb = pl.pallas_call(
        _bstate_kernel,
        grid=(nb,),
        in_specs=[_smem_spec(),
                  pl.BlockSpec((tb, RET_QK_W), lambda i: (nb - 1 - i, 0)),
                  pl.BlockSpec((tb, RET_V_W), lambda i: (nb - 1 - i, 0))],
        out_specs=pl.BlockSpec((cb, RET_HEADS, RET_DK, RET_DV), lambda i: (nb - 1 - i, 0, 0, 0)),
        out_shape=jax.ShapeDtypeStruct((n_chunks, RET_HEADS, RET_DK, RET_DV), BF16),
        scratch_shapes=[pltpu.VMEM((RET_HEADS, RET_DK, RET_DV), F32)],
        compiler_params=params(dimension_semantics=("arbitrary",)),
        name="bstate",
    )(lg, rk, rv)

    tm = MIX_ROWS
    cm = tm // CHUNK
    rowm = lambda w: pl.BlockSpec((tm, w), lambda i: (i, 0))
    halo_prev = pl.BlockSpec((CHUNK, KV_EXT_W), lambda i: (jnp.maximum(i * cm - 1, 0), 0))
    halo_next = pl.BlockSpec((CHUNK, KV_EXT_W), lambda i: (jnp.minimum(i * cm + cm, n_chunks - 1), 0))
    out = pl.pallas_call(
        functools.partial(_mix_kernel, seq_len=S),
        grid=(S // tm,),
        in_specs=[_smem_spec(), _smem_spec(),
                  rowm(D_MODEL), rowm(RET_QK_W), rowm(RET_QK_W), rowm(RET_V_W), rowm(RET_V_W),
                  pl.BlockSpec((cm, RET_HEADS, RET_DK, RET_DV), lambda i: (i, 0, 0, 0)),
                  rowm(ATT_Q_W), halo_prev, rowm(KV_EXT_W), halo_next,
                  halo_prev, rowm(KV_EXT_W), halo_next, rowm(ATT_Q_W),
                  rowm(D_MODEL), rowm(D_MODEL),
                  _const_spec((RET_V_W, D_MODEL)), _const_spec((ATT_Q_W, D_MODEL)),
                  _const_spec((D_MODEL, D_MODEL))],
        out_specs=rowm(D_MODEL),
        out_shape=jax.ShapeDtypeStruct((S, D_MODEL), F32),
        scratch_shapes=[pltpu.VMEM((RET_HEADS, RET_DK, RET_DV), F32),
                        pltpu.VMEM((tm + 2 * CHUNK, KV_EXT_W), BF16),
                        pltpu.VMEM((tm + 2 * CHUNK, KV_EXT_W), BF16),
                        pltpu.VMEM((tm, RET_V_W), BF16),
                        pltpu.VMEM((tm, ATT_Q_W), BF16)],
        compiler_params=params(dimension_semantics=("arbitrary",)),
        name="mix",
    )(lg, sink, x2, rq, rk, rv, rg, sb, aq, akx, akx, akx, avx, avx, avx, ag, mr, ma,
      w_ret_o.astype(BF16), w_attn_o.astype(BF16), w_out.astype(BF16))
    return out


def kernel(x, norm_g, w_in, ret_log_decay, q_norm_g, k_norm_g, attn_sink, w_ret_o, w_attn_o, w_out):
    B, S, D = x.shape
    assert D == D_MODEL and S % max(PROJ_ROWS, BSTATE_ROWS, MIX_ROWS) == 0
    assert B == 1
    xb = x.reshape(S, D)
    for l in range(norm_g.shape[0]):
        xb = _layer(xb, norm_g[l], w_in[l], ret_log_decay[l], q_norm_g[l], k_norm_g[l],
                    attn_sink[l], w_ret_o[l], w_attn_o[l], w_out[l])
    return xb.reshape(B, S, D)
```

```python
import functools

import jax
import jax.numpy as jnp
from jax.experimental import pallas as pl
from jax.experimental.pallas import tpu as pltpu

D_MODEL = 1024
RET_HEADS = 4
RET_DK = 128
RET_DV = 256
CHUNK = 128
ATT_Q_HEADS = 16
ATT_KV_HEADS = 4
ATT_HEAD_DIM = 64
RET_QK_W = RET_HEADS * RET_DK
RET_V_W = RET_HEADS * RET_DV
ATT_Q_W = ATT_Q_HEADS * ATT_HEAD_DIM
ATT_KV_W = ATT_KV_HEADS * ATT_HEAD_DIM
KV_EXT_W = ATT_KV_HEADS * 2 * 128
RMS_EPS = 1e-6
GN_EPS = 1e-6

_REF_OFF = {}
_o = 0
for _name, _w in (("rq", RET_QK_W), ("rk", RET_QK_W), ("rv", RET_V_W), ("rg", RET_V_W),
                  ("aq", ATT_Q_W), ("ak", ATT_KV_W), ("av", ATT_KV_W), ("ag", ATT_Q_W),
                  ("mr", D_MODEL), ("ma", D_MODEL)):
    _REF_OFF[_name] = (_o, _o + _w)
    _o += _w
D_IN = _o

_MIX_COLS = ("rq", "rg", "aq", "ak", "av", "ag", "mr", "ma")
_MIX_OFF = {}
_o = 0
for _name in _MIX_COLS:
    _w = _REF_OFF[_name][1] - _REF_OFF[_name][0]
    _MIX_OFF[_name] = (_o, _o + _w)
    _o += _w
MIX_W = _o

BSTATE_ROWS = 512
MIX_ROWS = 256
VMEM_LIMIT_BYTES = 56 * 1024 * 1024
NEG_BIAS = 3 * ATT_Q_HEADS

BF16 = jnp.bfloat16
F32 = jnp.float32


def _dot(a, b):
    return jnp.dot(a, b, preferred_element_type=F32)


def _dot_nt(a, b):
    return jax.lax.dot_general(a, b, (((1,), (1,)), ((), ())), preferred_element_type=F32)


def _dot_tn(a, b):
    return jax.lax.dot_general(a, b, (((0,), (0,)), ((), ())), preferred_element_type=F32)


def _rmsnorm_bf16(x, gain):
    ms = jnp.mean(x * x, axis=-1, keepdims=True)
    return (x * jax.lax.rsqrt(ms + RMS_EPS) * gain).astype(BF16)


def _split_hi_lo(x):
    hi = x.astype(BF16)
    lo = (x - hi.astype(F32)).astype(BF16)
    return hi, lo


def _head_rmsnorm(x, seg_mean, gain):
    outs = []
    for j in range(x.shape[1] // 256):
        xs = x[:, j * 256:(j + 1) * 256]
        hi, lo = _split_hi_lo(xs * xs)
        ms = _dot(hi, seg_mean) + _dot(lo, seg_mean)
        outs.append(xs * jax.lax.rsqrt(ms + RMS_EPS))
    y = outs[0] if len(outs) == 1 else jnp.concatenate(outs, axis=1)
    return y * gain


def _kv_extend(x):
    lane = jax.lax.broadcasted_iota(jnp.int32, (x.shape[0], 128), 1)
    low = lane < 64
    zero = jnp.zeros((x.shape[0], 128), F32)
    parts = []
    for j in range(2):
        col = x[:, j * 128:(j + 1) * 128]
        rolled = pltpu.roll(col, 64, axis=1)
        parts += [jnp.where(low, col, zero), jnp.where(low, zero, rolled),
                  jnp.where(low, rolled, zero), jnp.where(low, zero, col)]
    return jnp.concatenate(parts, axis=1)


def _sigmoid(x):
    return 0.5 * jnp.tanh(0.5 * x) + 0.5


def _silu(x):
    return x * _sigmoid(x)


def _row_pos(shape):
    return jax.lax.broadcasted_iota(jnp.int32, shape, 0).astype(F32)


def _bstate_kernel(lg_ref, x_ref, g_ref, wkv_ref, rk_ref, rv_ref, sb_ref, state_ref, kdec_ref):
    @pl.when(pl.program_id(0) == 0)
    def _():
        state_ref[...] = jnp.zeros_like(state_ref)
        pos = _row_pos((CHUNK, RET_DK))
        for hd in range(RET_HEADS):
            kdec_ref[hd] = jnp.exp(lg_ref[1, hd] * pos)

    h = _rmsnorm_bf16(x_ref[...], g_ref[...])
    rk = (_dot(h, wkv_ref[:, :RET_QK_W]) * (RET_DK ** -0.5)).astype(BF16)
    rv = _dot(h, wkv_ref[:, RET_QK_W:]).astype(BF16)
    rk_ref[...] = rk
    rv_ref[...] = rv

    for c in range(BSTATE_ROWS // CHUNK - 1, -1, -1):
        rows = slice(c * CHUNK, (c + 1) * CHUNK)
        for hd in range(RET_HEADS):
            st = state_ref[hd]
            sb_ref[c, hd] = st.astype(BF16)
            k = rk[rows, hd * RET_DK:(hd + 1) * RET_DK].astype(F32)
            kb = (k * kdec_ref[hd]).astype(BF16)
            kv = _dot_tn(kb, rv[rows, hd * RET_DV:(hd + 1) * RET_DV])
            decay = jnp.exp(jnp.full((1, RET_DV), lg_ref[1, hd] * float(CHUNK), F32))
            state_ref[hd] = st * decay + kv


def _init_tables(lg_ref, bias_ref, dmat_ref, qdf_ref, qdb_ref, kdf_ref):
    row_i = jax.lax.broadcasted_iota(jnp.int32, (CHUNK, CHUNK), 0)
    col_i = jax.lax.broadcasted_iota(jnp.int32, (CHUNK, CHUNK), 1)
    dist = (row_i - col_i).astype(F32)
    pos = _row_pos((CHUNK, RET_DK))
    for hd in range(RET_HEADS):
        lgf = lg_ref[0, hd]
        lgb = lg_ref[1, hd]
        dmat_ref[hd] = jnp.where(dist >= 0.0, jnp.exp(lgf * jnp.maximum(dist, 0.0)),
                                 jnp.exp(lgb * jnp.maximum(-dist, 0.0)))
        qdf_ref[hd] = jnp.exp(lgf * (pos + 1.0))
        qdb_ref[hd] = jnp.exp(lgb * (float(CHUNK) - pos))
        kdf_ref[hd] = jnp.exp(lgf * (float(CHUNK - 1) - pos))
    neg = jnp.full((CHUNK, CHUNK), -jnp.inf, F32)
    for head in range(ATT_Q_HEADS):
        slope = 2.0 ** (-8.0 * (head + 1) / ATT_Q_HEADS)
        for b in range(3):
            rel_i = jnp.abs(col_i + (b - 1) * CHUNK - row_i)
            bias_ref[3 * head + b] = jnp.where(rel_i <= CHUNK, -slope * rel_i.astype(F32), neg)
    bias_ref[NEG_BIAS] = neg


def _mix_kernel(lg_ref, sink_ref,
                x_ref, xn_ref, g_ref, w_ref, segm_ref, qg_ref, kg_ref,
                rk_ref, rv_ref, sb_ref, wro_ref, wao_ref, wout_ref,
                out_ref,
                sf_ref, kbuf_ref, vbuf_ref, h_ref, rq_ref, rg_ref, aq_ref, ag_ref, ret_ref, att_ref,
                bias_ref, dmat_ref, qdf_ref, qdb_ref, kdf_ref, *, n_steps):
    step = pl.program_id(0)
    n_chunks = MIX_ROWS // CHUNK

    @pl.when(step == 0)
    def _():
        sf_ref[...] = jnp.zeros_like(sf_ref)
        kbuf_ref[0:CHUNK] = jnp.zeros((CHUNK, KV_EXT_W), BF16)
        vbuf_ref[0:CHUNK] = jnp.zeros((CHUNK, KV_EXT_W), BF16)
        _init_tables(lg_ref, bias_ref, dmat_ref, qdf_ref, qdb_ref, kdf_ref)

    def w(name):
        a, b = _MIX_OFF[name]
        return w_ref[:, a:b]

    g = g_ref[...]
    h = _rmsnorm_bf16(x_ref[...], g)
    h_ref[...] = h
    segm = segm_ref[...]

    h_ext = jnp.concatenate([h, _rmsnorm_bf16(xn_ref[...], g)], axis=0)
    kbuf_ref[CHUNK:] = _kv_extend(_head_rmsnorm(_dot(h_ext, w("ak")), segm, kg_ref[...])).astype(BF16)
    vbuf_ref[CHUNK:] = _kv_extend(_dot(h_ext, w("av"))).astype(BF16)

    aq_ref[...] = _head_rmsnorm(_dot(h, w("aq")), segm,
                                qg_ref[...] * (ATT_HEAD_DIM ** -0.5)).astype(BF16)
    ag_ref[...] = _silu(_dot(h, w("ag")))
    rq_ref[...] = _dot(h, w("rq")).astype(BF16)
    rg_ref[...] = _silu(_dot(h, w("rg")))

    lane128 = jax.lax.broadcasted_iota(jnp.int32, (CHUNK, 128), 1)

    for c in range(n_chunks):
        rows = slice(c * CHUNK, (c + 1) * CHUNK)

        for hd in range(RET_HEADS):
            kcols = slice(hd * RET_DK, (hd + 1) * RET_DK)
            vcols = slice(hd * RET_DV, (hd + 1) * RET_DV)
            q = rq_ref[rows, kcols]
            k = rk_ref[rows, kcols]
            v = rv_ref[rows, vcols]
            qf = q.astype(F32)
            scores = _dot_nt(q, k) * dmat_ref[hd]
            s_f = sf_ref[hd]
            lhs = jnp.concatenate([scores, qf * qdf_ref[hd], qf * qdb_ref[hd]], axis=1).astype(BF16)
            rhs = jnp.concatenate([v, s_f.astype(BF16), sb_ref[c, hd]], axis=0)
            o = _dot(lhs, rhs)
            k_dec = (k.astype(F32) * kdf_ref[hd]).astype(BF16)
            chunk_decay = jnp.exp(jnp.full((1, RET_DV), lg_ref[0, hd] * float(CHUNK), F32))
            sf_ref[hd] = s_f * chunk_decay + _dot_tn(k_dec, v)
            mu = jnp.mean(o, axis=-1, keepdims=True)
            d = o - mu
            var = jnp.mean(d * d, axis=-1, keepdims=True)
            y = d * jax.lax.rsqrt(var + GN_EPS)
            ret_ref[rows, vcols] = (rg_ref[rows, vcols] * y).astype(BF16)

        first = jnp.logical_and(step == 0, c == 0)
        last = jnp.logical_and(step == n_steps - 1, c == n_chunks - 1)
        win = slice(c * CHUNK, (c + 3) * CHUNK)
        for grp in range(ATT_KV_HEADS):
            lo = slice(grp * 256, grp * 256 + 128)
            hi = slice(grp * 256 + 128, grp * 256 + 256)
            kk = jnp.concatenate([kbuf_ref[win, lo], kbuf_ref[win, hi]], axis=0)
            vv = jnp.concatenate([vbuf_ref[win, lo], vbuf_ref[win, hi]], axis=0)
            q2 = jnp.concatenate([aq_ref[rows, lo], aq_ref[rows, hi]], axis=0)
            s_all = _dot_nt(q2, kk)
            p_rows = []
            inv_rows = []
            for r in range(2):
                p_cols = []
                invs = []
                for e in range(2):
                    head = grp * 4 + r * 2 + e
                    sink = sink_ref[head]
                    idx = (jnp.where(first, NEG_BIAS, 3 * head), 3 * head + 1,
                           jnp.where(last, NEG_BIAS, 3 * head + 2))
                    sb3 = [s_all[r * CHUNK:(r + 1) * CHUNK,
                                 (3 * e + b) * CHUNK:(3 * e + b + 1) * CHUNK] + bias_ref[idx[b]]
                           for b in range(3)]
                    m = jnp.max(jnp.maximum(jnp.maximum(sb3[0], sb3[1]), sb3[2]),
                                axis=-1, keepdims=True)
                    m = jnp.maximum(m, sink)
                    p3 = [jnp.exp(s - m) for s in sb3]
                    denom = jnp.sum(p3[0] + p3[1] + p3[2], axis=-1, keepdims=True) + jnp.exp(sink - m)
                    p_cols += [p.astype(BF16) for p in p3]
                    invs.append(1.0 / denom)
                p_rows.append(jnp.concatenate(p_cols, axis=1))
                inv_rows.append(jnp.where(lane128 < 64, invs[0], invs[1]))
            p_all = jnp.concatenate(p_rows, axis=0)
            o2 = _dot(p_all, vv)
            for r in range(2):
                cols = slice(grp * 256 + r * 128, grp * 256 + (r + 1) * 128)
                att = o2[r * CHUNK:(r + 1) * CHUNK] * inv_rows[r]
                att_ref[rows, cols] = (ag_ref[rows, cols] * att).astype(BF16)

    kbuf_ref[0:CHUNK] = kbuf_ref[MIX_ROWS:MIX_ROWS + CHUNK]
    vbuf_ref[0:CHUNK] = vbuf_ref[MIX_ROWS:MIX_ROWS + CHUNK]

    h = h_ref[...]
    y_r = _dot(ret_ref[...], wro_ref[...])
    y_a = _dot(att_ref[...], wao_ref[...])
    merged = (_sigmoid(_dot(h, w("mr"))) * y_r + _sigmoid(_dot(h, w("ma"))) * y_a).astype(BF16)
    out_ref[...] = x_ref[...] + _dot(merged, wout_ref[...])


def _const_spec(shape):
    return pl.BlockSpec(shape, lambda i: (0,) * len(shape), pipeline_mode=pl.Buffered(1))


def _smem_spec():
    return pl.BlockSpec(memory_space=pltpu.SMEM)


def _seg_mean_matrix():
    r = jnp.arange(256)[:, None] // ATT_HEAD_DIM
    c = jnp.arange(256)[None, :] // ATT_HEAD_DIM
    return jnp.where(r == c, 1.0 / ATT_HEAD_DIM, 0.0).astype(BF16)


def _layer(x2, norm_g, w_in, lg, q_g, k_g, sink, w_ret_o, w_attn_o, w_out):
    S = x2.shape[0]
    n_chunks = S // CHUNK
    params = functools.partial(pltpu.CompilerParams, vmem_limit_bytes=VMEM_LIMIT_BYTES)
    gain = norm_g.reshape(1, D_MODEL)
    w_bf = w_in.astype(BF16)
    w_kv = w_bf[:, _REF_OFF["rk"][0]:_REF_OFF["rv"][1]]
    w_mix = jnp.concatenate([w_bf[:, _REF_OFF[n][0]:_REF_OFF[n][1]] for n in _MIX_COLS], axis=1)

    tb = BSTATE_ROWS
    nb = S // tb
    cb = tb // CHUNK
    rk, rv, sb = pl.pallas_call(
        _bstate_kernel,
        grid=(nb,),
        in_specs=[_smem_spec(),
                  pl.BlockSpec((tb, D_MODEL), lambda i: (nb - 1 - i, 0)),
                  _const_spec((1, D_MODEL)), _const_spec((D_MODEL, RET_QK_W + RET_V_W))],
        out_specs=[pl.BlockSpec((tb, RET_QK_W), lambda i: (nb - 1 - i, 0)),
                   pl.BlockSpec((tb, RET_V_W), lambda i: (nb - 1 - i, 0)),
                   pl.BlockSpec((cb, RET_HEADS, RET_DK, RET_DV), lambda i: (nb - 1 - i, 0, 0, 0))],
        out_shape=[jax.ShapeDtypeStruct((S, RET_QK_W), BF16),
                   jax.ShapeDtypeStruct((S, RET_V_W), BF16),
                   jax.ShapeDtypeStruct((n_chunks, RET_HEADS, RET_DK, RET_DV), BF16)],
        scratch_shapes=[pltpu.VMEM((RET_HEADS, RET_DK, RET_DV), F32),
                        pltpu.VMEM((RET_HEADS, CHUNK, RET_DK), F32)],
        compiler_params=params(dimension_semantics=("arbitrary",)),
        name="bstate",
    )(lg, x2, gain, w_kv)

    tm = MIX_ROWS
    cm = tm // CHUNK
    n_steps = S // tm
    rowm = lambda width: pl.BlockSpec((tm, width), lambda i: (i, 0))
    table = lambda n: pltpu.VMEM((n, CHUNK, CHUNK), F32)
    out = pl.pallas_call(
        functools.partial(_mix_kernel, n_steps=n_steps),
        grid=(n_steps,),
        in_specs=[_smem_spec(), _smem_spec(),
                  rowm(D_MODEL),
                  pl.BlockSpec((CHUNK, D_MODEL), lambda i: (jnp.minimum(i * cm + cm, n_chunks - 1), 0)),
                  _const_spec((1, D_MODEL)), _const_spec((D_MODEL, MIX_W)), _const_spec((256, 256)),
                  _const_spec((1, ATT_Q_W)), _const_spec((1, ATT_KV_W)),
                  rowm(RET_QK_W), rowm(RET_V_W),
                  pl.BlockSpec((cm, RET_HEADS, RET_DK, RET_DV), lambda i: (i, 0, 0, 0)),
                  _const_spec((RET_V_W, D_MODEL)), _const_spec((ATT_Q_W, D_MODEL)),
                  _const_spec((D_MODEL, D_MODEL))],
        out_specs=rowm(D_MODEL),
        out_shape=jax.ShapeDtypeStruct((S, D_MODEL), F32),
        scratch_shapes=[pltpu.VMEM((RET_HEADS, RET_DK, RET_DV), F32),
                        pltpu.VMEM((tm + 2 * CHUNK, KV_EXT_W), BF16),
                        pltpu.VMEM((tm + 2 * CHUNK, KV_EXT_W), BF16),
                        pltpu.VMEM((tm, D_MODEL), BF16),
                        pltpu.VMEM((tm, RET_QK_W), BF16),
                        pltpu.VMEM((tm, RET_V_W), F32),
                        pltpu.VMEM((tm, ATT_Q_W), BF16),
                        pltpu.VMEM((tm, ATT_Q_W), F32),
                        pltpu.VMEM((tm, RET_V_W), BF16),
                        pltpu.VMEM((tm, ATT_Q_W), BF16),
                        table(NEG_BIAS + 1), table(RET_HEADS), table(RET_HEADS), table(RET_HEADS),
                        table(RET_HEADS)],
        compiler_params=params(dimension_semantics=("arbitrary",)),
        name="mix",
    )(lg, sink, x2, x2, gain, w_mix, _seg_mean_matrix(),
      jnp.tile(q_g.reshape(1, ATT_HEAD_DIM), (1, ATT_Q_HEADS)),
      jnp.tile(k_g.reshape(1, ATT_HEAD_DIM), (1, ATT_KV_HEADS)),
      rk, rv, sb, w_ret_o.astype(BF16), w_attn_o.astype(BF16), w_out.astype(BF16))
    return out


def kernel(x, norm_g, w_in, ret_log_decay, q_norm_g, k_norm_g, attn_sink, w_ret_o, w_attn_o, w_out):
    B, S, D = x.shape
    assert B == 1 and D == D_MODEL and S % max(BSTATE_ROWS, MIX_ROWS) == 0
    xb = x.reshape(S, D)
    for l in range(norm_g.shape[0]):
        xb = _layer(xb, norm_g[l], w_in[l], ret_log_decay[l], q_norm_g[l], k_norm_g[l],
                    attn_sink[l], w_ret_o[l], w_attn_o[l], w_out[l])
    return xb.reshape(B, S, D)
```

```python
import functools
from typing import NamedTuple, Any

import jax
import jax.numpy as jnp
from jax.experimental import pallas as pl
from jax.experimental.pallas import tpu as pltpu

D_MODEL = 1024
RET_HEADS = 4
RET_DK = 128
RET_DV = 256
CHUNK = 128
ATT_Q_HEADS = 16
ATT_KV_HEADS = 4
ATT_HEAD_DIM = 64
RET_QK_W = RET_HEADS * RET_DK
RET_V_W = RET_HEADS * RET_DV
ATT_Q_W = ATT_Q_HEADS * ATT_HEAD_DIM
ATT_KV_W = ATT_KV_HEADS * ATT_HEAD_DIM
KV_EXT_W = ATT_KV_HEADS * 2 * 128
RMS_EPS = 1e-6
GN_EPS = 1e-6

_REF_OFF = {}
_o = 0
for _name, _w in (("rq", RET_QK_W), ("rk", RET_QK_W), ("rv", RET_V_W), ("rg", RET_V_W),
                  ("aq", ATT_Q_W), ("ak", ATT_KV_W), ("av", ATT_KV_W), ("ag", ATT_Q_W),
                  ("mr", D_MODEL), ("ma", D_MODEL)):
    _REF_OFF[_name] = (_o, _o + _w)
    _o += _w
D_IN = _o

_MIX_COLS = ("ak", "av", "rq", "rg", "aq", "ag", "mr", "ma")
_MIX_OFF = {}
_o = 0
for _name in _MIX_COLS:
    _w = _REF_OFF[_name][1] - _REF_OFF[_name][0]
    _MIX_OFF[_name] = (_o, _o + _w)
    _o += _w
MIX_W = _o

BSTATE_ROWS = 512
MIX_ROWS = 2 * CHUNK
VMEM_LIMIT_BYTES = 56 * 1024 * 1024
NEG_BIAS = 3 * ATT_Q_HEADS

BF16 = jnp.bfloat16
F32 = jnp.float32


def _dot(a, b):
    return jnp.dot(a, b, preferred_element_type=F32)


def _dot_nt(a, b):
    return jax.lax.dot_general(a, b, (((1,), (1,)), ((), ())), preferred_element_type=F32)


def _dot_tn(a, b):
    return jax.lax.dot_general(a, b, (((0,), (0,)), ((), ())), preferred_element_type=F32)


def _rmsnorm_bf16(x, gain):
    ms = jnp.mean(x * x, axis=-1, keepdims=True)
    return (x * jax.lax.rsqrt(ms + RMS_EPS) * gain).astype(BF16)


def _split_hi_lo(x):
    hi = x.astype(BF16)
    lo = (x - hi.astype(F32)).astype(BF16)
    return hi, lo


def _head_rmsnorm(x, seg_mean, gain):
    outs = []
    for j in range(x.shape[1] // 256):
        xs = x[:, j * 256:(j + 1) * 256]
        hi, lo = _split_hi_lo(xs * xs)
        ms = _dot(hi, seg_mean) + _dot(lo, seg_mean)
        outs.append(xs * jax.lax.rsqrt(ms + RMS_EPS))
    y = outs[0] if len(outs) == 1 else jnp.concatenate(outs, axis=1)
    return y * gain


def _kv_extend(x):
    lane = jax.lax.broadcasted_iota(jnp.int32, (x.shape[0], 128), 1)
    low = lane < 64
    zero = jnp.zeros((x.shape[0], 128), F32)
    parts = []
    for j in range(2):
        col = x[:, j * 128:(j + 1) * 128]
        rolled = pltpu.roll(col, 64, axis=1)
        parts += [jnp.where(low, col, zero), jnp.where(low, zero, rolled),
                  jnp.where(low, rolled, zero), jnp.where(low, zero, col)]
    return jnp.concatenate(parts, axis=1)


def _sigmoid(x):
    return 0.5 * jnp.tanh(0.5 * x) + 0.5


def _silu(x):
    return x * _sigmoid(x)


def _row_pos(shape):
    return jax.lax.broadcasted_iota(jnp.int32, shape, 0).astype(F32)


def _bstate_kernel(lg_ref, x_ref, g_ref, wkv_ref, rk_ref, rv_ref, sb_ref, state_ref, kdec_ref):
    @pl.when(pl.program_id(0) == 0)
    def _():
        state_ref[...] = jnp.zeros_like(state_ref)
        pos = _row_pos((CHUNK, RET_DK))
        for hd in range(RET_HEADS):
            kdec_ref[hd] = jnp.exp(lg_ref[1, hd] * pos)

    h = _rmsnorm_bf16(x_ref[...], g_ref[...])
    rk = (_dot(h, wkv_ref[:, :RET_QK_W]) * (RET_DK ** -0.5)).astype(BF16)
    rv = _dot(h, wkv_ref[:, RET_QK_W:]).astype(BF16)
    rk_ref[...] = rk
    rv_ref[...] = rv

    for c in range(BSTATE_ROWS // CHUNK - 1, -1, -1):
        rows = slice(c * CHUNK, (c + 1) * CHUNK)
        for hd in range(RET_HEADS):
            st = state_ref[hd]
            sb_ref[c, hd] = st.astype(BF16)
            k = rk[rows, hd * RET_DK:(hd + 1) * RET_DK].astype(F32)
            kb = (k * kdec_ref[hd]).astype(BF16)
            kv = _dot_tn(kb, rv[rows, hd * RET_DV:(hd + 1) * RET_DV])
            decay = jnp.exp(jnp.full((1, RET_DV), lg_ref[1, hd] * float(CHUNK), F32))
            state_ref[hd] = st * decay + kv


class _BlockSet(NamedTuple):
    aq: Any
    ag: Any
    rq: Any
    rg: Any
    smr: Any
    sma: Any
    kx: Any
    vx: Any


class _Tables(NamedTuple):
    bias: Any
    dmat: Any
    qdf: Any
    qdb: Any
    kdf: Any


def _init_tables(lg_ref, t):
    row_i = jax.lax.broadcasted_iota(jnp.int32, (CHUNK, CHUNK), 0)
    col_i = jax.lax.broadcasted_iota(jnp.int32, (CHUNK, CHUNK), 1)
    dist = (row_i - col_i).astype(F32)
    pos = _row_pos((CHUNK, RET_DK))
    for hd in range(RET_HEADS):
        lgf = lg_ref[0, hd]
        lgb = lg_ref[1, hd]
        t.dmat[hd] = jnp.where(dist >= 0.0, jnp.exp(lgf * jnp.maximum(dist, 0.0)),
                               jnp.exp(lgb * jnp.maximum(-dist, 0.0)))
        t.qdf[hd] = jnp.exp(lgf * (pos + 1.0))
        t.qdb[hd] = jnp.exp(lgb * (float(CHUNK) - pos))
        t.kdf[hd] = jnp.exp(lgf * (float(CHUNK - 1) - pos))
    neg = jnp.full((CHUNK, CHUNK), -jnp.inf, F32)
    for head in range(ATT_Q_HEADS):
        slope = 2.0 ** (-8.0 * (head + 1) / ATT_Q_HEADS)
        for b in range(3):
            rel_i = jnp.abs(col_i + (b - 1) * CHUNK - row_i)
            t.bias[3 * head + b] = jnp.where(rel_i <= CHUNK, -slope * rel_i.astype(F32), neg)
    t.bias[NEG_BIAS] = neg


def _stage_a_pieces(x_ref, g_ref, w_ref, segm_ref, qg_ref, kg_ref, dst, knext_ref, vnext_ref):
    TILE = 256
    state = {}

    def head():
        h = _rmsnorm_bf16(x_ref[...], g_ref[...])
        state["h"] = h
        segm = segm_ref[...]
        a, b = _MIX_OFF["ak"]
        kx = _kv_extend(_head_rmsnorm(_dot(h, w_ref[:, a:b]), segm, kg_ref[...])).astype(BF16)
        dst.kx[...] = kx
        knext_ref[...] = kx[0:CHUNK]
        a, b = _MIX_OFF["av"]
        vx = _kv_extend(_dot(h, w_ref[:, a:b])).astype(BF16)
        dst.vx[...] = vx
        vnext_ref[...] = vx[0:CHUNK]

    def piece(name, j, ref, post):
        def run():
            a = _MIX_OFF[name][0] + j * TILE
            cols = slice(j * TILE, (j + 1) * TILE)
            ref[:, cols] = post(_dot(state["h"], w_ref[:, a:a + TILE]), cols)
        return run

    def q_norm(y, cols):
        gain = qg_ref[:, cols] * (ATT_HEAD_DIM ** -0.5)
        return _head_rmsnorm(y, segm_ref[...], gain).astype(BF16)

    plan = (("rq", dst.rq, lambda y, cols: y.astype(BF16)),
            ("aq", dst.aq, q_norm),
            ("rg", dst.rg, lambda y, cols: _silu(y)),
            ("ag", dst.ag, lambda y, cols: _silu(y)),
            ("mr", dst.smr, lambda y, cols: _sigmoid(y)),
            ("ma", dst.sma, lambda y, cols: _sigmoid(y)))
    pieces = []
    for name, ref, post in plan:
        width = _MIX_OFF[name][1] - _MIX_OFF[name][0]
        pieces += [piece(name, j, ref, post) for j in range(width // TILE)]
    return head, pieces


def _stage_b_units(src, lg_ref, sink_ref, rk_ref, rv_ref, sb_ref, sf_ref,
                   kprev_ref, vprev_ref, knext_ref, vnext_ref, ret_ref, att_ref, t,
                   first_block, last_block):
    n_chunks = MIX_ROWS // CHUNK
    lane128 = jax.lax.broadcasted_iota(jnp.int32, (CHUNK, 128), 1)

    def chunk_rows(c):
        return slice(c * CHUNK, (c + 1) * CHUNK)

    def window(own, prev_ref, next_ref, c, cols):
        parts = []
        for cc in (c - 1, c, c + 1):
            if cc < 0:
                parts.append(prev_ref[:, cols])
            elif cc >= n_chunks:
                parts.append(next_ref[:, cols])
            else:
                parts.append(own[chunk_rows(cc), cols])
        return parts

    def retention_unit(c, hd):
        rows = chunk_rows(c)
        kcols = slice(hd * RET_DK, (hd + 1) * RET_DK)
        vcols = slice(hd * RET_DV, (hd + 1) * RET_DV)
        st = {}

        def s1():
            q = src.rq[rows, kcols]
            k = rk_ref[rows, kcols]
            qf = q.astype(F32)
            scores = _dot_nt(q, k) * t.dmat[hd]
            st["lhs"] = jnp.concatenate([scores, qf * t.qdf[hd], qf * t.qdb[hd]], axis=1).astype(BF16)

        def s2():
            k = rk_ref[rows, kcols]
            v = rv_ref[rows, vcols]
            s_f = sf_ref[hd]
            rhs = jnp.concatenate([v, s_f.astype(BF16), sb_ref[c, hd]], axis=0)
            st["o"] = _dot(st["lhs"], rhs)
            k_dec = (k.astype(F32) * t.kdf[hd]).astype(BF16)
            chunk_decay = jnp.exp(jnp.full((1, RET_DV), lg_ref[0, hd] * float(CHUNK), F32))
            sf_ref[hd] = s_f * chunk_decay + _dot_tn(k_dec, v)

        def s3():
            o = st["o"]
            mu = jnp.mean(o, axis=-1, keepdims=True)
            d = o - mu
            var = jnp.mean(d * d, axis=-1, keepdims=True)
            y = d * jax.lax.rsqrt(var + GN_EPS)
            ret_ref[rows, vcols] = (src.rg[rows, vcols] * y).astype(BF16)

        return s1, s2, s3

    def attention_unit(c, grp):
        rows = chunk_rows(c)
        lo = slice(grp * 256, grp * 256 + 128)
        hi = slice(grp * 256 + 128, grp * 256 + 256)
        first = jnp.logical_and(first_block, c == 0)
        last = jnp.logical_and(last_block, c == n_chunks - 1)
        st = {}

        def s1():
            kk = jnp.concatenate(window(src.kx, kprev_ref, knext_ref, c, lo)
                                 + window(src.kx, kprev_ref, knext_ref, c, hi), axis=0)
            q2 = jnp.concatenate([src.aq[rows, lo], src.aq[rows, hi]], axis=0)
            st["s"] = _dot_nt(q2, kk)

        def s2():
            s_all = st["s"]
            p_rows = []
            inv_rows = []
            for r in range(2):
                p_cols = []
                invs = []
                for e in range(2):
                    head = grp * 4 + r * 2 + e
                    sink = sink_ref[head]
                    idx = (jnp.where(first, NEG_BIAS, 3 * head), 3 * head + 1,
                           jnp.where(last, NEG_BIAS, 3 * head + 2))
                    sb3 = [s_all[r * CHUNK:(r + 1) * CHUNK,
                                 (3 * e + b) * CHUNK:(3 * e + b + 1) * CHUNK] + t.bias[idx[b]]
                           for b in range(3)]
                    m = jnp.max(jnp.maximum(jnp.maximum(sb3[0], sb3[1]), sb3[2]),
                                axis=-1, keepdims=True)
                    m = jnp.maximum(m, sink)
                    p3 = [jnp.exp(s - m) for s in sb3]
                    denom = jnp.sum(p3[0] + p3[1] + p3[2], axis=-1, keepdims=True) + jnp.exp(sink - m)
                    p_cols += [p.astype(BF16) for p in p3]
                    invs.append(1.0 / denom)
                p_rows.append(jnp.concatenate(p_cols, axis=1))
                inv_rows.append(jnp.where(lane128 < 64, invs[0], invs[1]))
            st["p"] = jnp.concatenate(p_rows, axis=0)
            st["inv"] = inv_rows

        def s3():
            vv = jnp.concatenate(window(src.vx, vprev_ref, vnext_ref, c, lo)
                                 + window(src.vx, vprev_ref, vnext_ref, c, hi), axis=0)
            o2 = _dot(st["p"], vv)
            for r in range(2):
                cols = slice(grp * 256 + r * 128, grp * 256 + (r + 1) * 128)
                att = o2[r * CHUNK:(r + 1) * CHUNK] * st["inv"][r]
                att_ref[rows, cols] = (src.ag[rows, cols] * att).astype(BF16)

        return s1, s2, s3

    units = []
    for c in range(n_chunks):
        for j in range(RET_HEADS):
            units.append(retention_unit(c, j))
            units.append(attention_unit(c, j))
    return units


def _emit_interleaved(units, fillers):
    n = len(units)
    slots = n + 2
    done = 0
    for i in range(slots):
        if i < n:
            units[i][0]()
        upto = (len(fillers) * (i + 1)) // slots
        for f in fillers[done:upto]:
            f()
        done = upto
        if 0 <= i - 1 < n:
            units[i - 1][1]()
        if 0 <= i - 2 < n:
            units[i - 2][2]()


def _mix_kernel(lg_ref, sink_ref,
                x_ref, xres_ref, g_ref, w_ref, segm_ref, qg_ref, kg_ref,
                rk_ref, rv_ref, sb_ref, wro_ref, wao_ref, wout_ref,
                out_ref, *scratch, n_blocks):
    n_set = len(_BlockSet._fields)
    sets = (_BlockSet(*scratch[:n_set]), _BlockSet(*scratch[n_set:2 * n_set]))
    (sf_ref, kprev_ref, vprev_ref, knext_ref, vnext_ref, ret_ref, att_ref) = scratch[2 * n_set:2 * n_set + 7]
    t = _Tables(*scratch[2 * n_set + 7:])
    step = pl.program_id(0)

    @pl.when(step == 0)
    def _():
        for ref in sets[1]:
            ref[...] = jnp.zeros_like(ref)
        sf_ref[...] = jnp.zeros_like(sf_ref)
        kprev_ref[...] = jnp.zeros_like(kprev_ref)
        vprev_ref[...] = jnp.zeros_like(vprev_ref)
        _init_tables(lg_ref, t)

    def body(dst, src):
        head, pieces = _stage_a_pieces(x_ref, g_ref, w_ref, segm_ref, qg_ref, kg_ref, dst,
                                       knext_ref, vnext_ref)
        units = _stage_b_units(src, lg_ref, sink_ref, rk_ref, rv_ref, sb_ref, sf_ref,
                               kprev_ref, vprev_ref, knext_ref, vnext_ref, ret_ref, att_ref, t,
                               first_block=step == 1, last_block=step == n_blocks)
        head()
        _emit_interleaved(units, pieces)
        kprev_ref[...] = src.kx[MIX_ROWS - CHUNK:, :]
        vprev_ref[...] = src.vx[MIX_ROWS - CHUNK:, :]
        y_r = _dot(ret_ref[...], wro_ref[...])
        y_a = _dot(att_ref[...], wao_ref[...])
        merged = (src.smr[...] * y_r + src.sma[...] * y_a).astype(BF16)
        out_ref[...] = xres_ref[...] + _dot(merged, wout_ref[...])

    @pl.when(step % 2 == 0)
    def _():
        body(sets[0], sets[1])

    @pl.when(step % 2 == 1)
    def _():
        body(sets[1], sets[0])

    @pl.when(step == 0)
    def _():
        sf_ref[...] = jnp.zeros_like(sf_ref)
        kprev_ref[...] = jnp.zeros_like(kprev_ref)
        vprev_ref[...] = jnp.zeros_like(vprev_ref)


def _const_spec(shape):
    return pl.BlockSpec(shape, lambda i: (0,) * len(shape), pipeline_mode=pl.Buffered(1))


def _smem_spec():
    return pl.BlockSpec(memory_space=pltpu.SMEM)


def _seg_mean_matrix():
    r = jnp.arange(256)[:, None] // ATT_HEAD_DIM
    c = jnp.arange(256)[None, :] // ATT_HEAD_DIM
    return jnp.where(r == c, 1.0 / ATT_HEAD_DIM, 0.0).astype(BF16)


def _layer(x2, norm_g, w_in, lg, q_g, k_g, sink, w_ret_o, w_attn_o, w_out):
    S = x2.shape[0]
    n_chunks = S // CHUNK
    params = functools.partial(pltpu.CompilerParams, vmem_limit_bytes=VMEM_LIMIT_BYTES)
    gain = norm_g.reshape(1, D_MODEL)
    w_bf = w_in.astype(BF16)
    w_kv = w_bf[:, _REF_OFF["rk"][0]:_REF_OFF["rv"][1]]
    w_mix = jnp.concatenate([w_bf[:, _REF_OFF[n][0]:_REF_OFF[n][1]] for n in _MIX_COLS], axis=1)

    tb = BSTATE_ROWS
    nb = S // tb
    cb = tb // CHUNK
    rk, rv, sb = pl.pallas_call(
        _bstate_kernel,
        grid=(nb,),
        in_specs=[_smem_spec(),
                  pl.BlockSpec((tb, D_MODEL), lambda i: (nb - 1 - i, 0)),
                  _const_spec((1, D_MODEL)), _const_spec((D_MODEL, RET_QK_W + RET_V_W))],
        out_specs=[pl.BlockSpec((tb, RET_QK_W), lambda i: (nb - 1 - i, 0)),
                   pl.BlockSpec((tb, RET_V_W), lambda i: (nb - 1 - i, 0)),
                   pl.BlockSpec((cb, RET_HEADS, RET_DK, RET_DV), lambda i: (nb - 1 - i, 0, 0, 0))],
        out_shape=[jax.ShapeDtypeStruct((S, RET_QK_W), BF16),
                   jax.ShapeDtypeStruct((S, RET_V_W), BF16),
                   jax.ShapeDtypeStruct((n_chunks, RET_HEADS, RET_DK, RET_DV), BF16)],
        scratch_shapes=[pltpu.VMEM((RET_HEADS, RET_DK, RET_DV), F32),
                        pltpu.VMEM((RET_HEADS, CHUNK, RET_DK), F32)],
        compiler_params=params(dimension_semantics=("arbitrary",)),
        name="bstate",
    )(lg, x2, gain, w_kv)

    tm = MIX_ROWS
    cm = tm // CHUNK
    n_blocks = S // tm
    ahead = lambda width: pl.BlockSpec((tm, width), lambda i: (jnp.minimum(i, n_blocks - 1), 0))
    behind = lambda width: pl.BlockSpec((tm, width), lambda i: (jnp.maximum(i - 1, 0), 0))
    table = lambda n: pltpu.VMEM((n, CHUNK, CHUNK), F32)
    block_set = [pltpu.VMEM((tm, ATT_Q_W), BF16), pltpu.VMEM((tm, ATT_Q_W), F32),
                 pltpu.VMEM((tm, RET_QK_W), BF16), pltpu.VMEM((tm, RET_V_W), F32),
                 pltpu.VMEM((tm, D_MODEL), F32), pltpu.VMEM((tm, D_MODEL), F32),
                 pltpu.VMEM((tm, KV_EXT_W), BF16), pltpu.VMEM((tm, KV_EXT_W), BF16)]
    halo = pltpu.VMEM((CHUNK, KV_EXT_W), BF16)
    out = pl.pallas_call(
        functools.partial(_mix_kernel, n_blocks=n_blocks),
        grid=(n_blocks + 1,),
        in_specs=[_smem_spec(), _smem_spec(),
                  ahead(D_MODEL), behind(D_MODEL),
                  _const_spec((1, D_MODEL)), _const_spec((D_MODEL, MIX_W)), _const_spec((256, 256)),
                  _const_spec((1, ATT_Q_W)), _const_spec((1, ATT_KV_W)),
                  behind(RET_QK_W), behind(RET_V_W),
                  pl.BlockSpec((cm, RET_HEADS, RET_DK, RET_DV), lambda i: (jnp.maximum(i - 1, 0), 0, 0, 0)),
                  _const_spec((RET_V_W, D_MODEL)), _const_spec((ATT_Q_W, D_MODEL)),
                  _const_spec((D_MODEL, D_MODEL))],
        out_specs=behind(D_MODEL),
        out_shape=jax.ShapeDtypeStruct((S, D_MODEL), F32),
        scratch_shapes=block_set + block_set + [
            pltpu.VMEM((RET_HEADS, RET_DK, RET_DV), F32),
            halo, halo, halo, halo,
            pltpu.VMEM((tm, RET_V_W), BF16),
            pltpu.VMEM((tm, ATT_Q_W), BF16),
            table(NEG_BIAS + 1), table(RET_HEADS), table(RET_HEADS), table(RET_HEADS),
            table(RET_HEADS)],
        compiler_params=params(dimension_semantics=("arbitrary",)),
        name="mix",
    )(lg, sink, x2, x2, gain, w_mix, _seg_mean_matrix(),
      jnp.tile(q_g.reshape(1, ATT_HEAD_DIM), (1, ATT_Q_HEADS)),
      jnp.tile(k_g.reshape(1, ATT_HEAD_DIM), (1, ATT_KV_HEADS)),
      rk, rv, sb, w_ret_o.astype(BF16), w_attn_o.astype(BF16), w_out.astype(BF16))
    return out


def kernel(x, norm_g, w_in, ret_log_decay, q_norm_g, k_norm_g, attn_sink, w_ret_o, w_attn_o, w_out):
    B, S, D = x.shape
    assert B == 1 and D == D_MODEL and S % max(BSTATE_ROWS, MIX_ROWS) == 0
    xb = x.reshape(S, D)
    for l in range(norm_g.shape[0]):
        xb = _layer(xb, norm_g[l], w_in[l], ret_log_decay[l], q_norm_g[l], k_norm_g[l],
                    attn_sink[l], w_ret_o[l], w_attn_o[l], w_out[l])
    return xb.reshape(B, S, D)
```
